```python
import math
import jax, jax.numpy as jnp
from jax import lax
import numpy as np

D_MODEL = 1024
BATCH = 8
SEQ = 2048
DEPTH = 4
DEC_BATCH = 128
DEC_SEQ = 4
PAST_LEN = 2048
PAGE_SIZE = 128

EPS = 1e-6
DH = 64
ROPE_DIMS = DH // 4
ROPE_THETA = 500000.0
A_HEADS = 8
A_WINDOWS = (128, 512, 2048)
A_DILATIONS = (1, 4, 16)
A_QBLOCK = 64
B_HEADS = 4
B_DK = 64
B_DV = 128
RET_THETA = 10000.0
RET_CHUNK = 128
D_INNER = 1024
C_HEADDIM = 64
C_HEADS = D_INNER // C_HEADDIM
C_GROUPS = 2
N_STATE = 128
CONV_W = 4
CONV_CH = D_INNER + 2 * C_GROUPS * N_STATE
SSD_CHUNK = 128
D_HEADS = 4
D_DH = 64
D_DV = 2 * D_DH
D_QBLOCK = 128
PEER_HEADS = 8
N_KEYS = 128
N_EXPERTS = N_KEYS * N_KEYS
PEER_TOPK = 16
D_KEY = 256
PEER_BLOCK = 128

N_EVEN = (DEPTH + 1) // 2
N_ODD = DEPTH // 2
EVEN_SPLITS = (3 * A_HEADS * DH,) * 3 + (B_HEADS * B_DK, B_HEADS * B_DK, B_HEADS * B_DV, B_HEADS * B_DV)
EVEN_OUT = A_HEADS * DH + B_HEADS * B_DV
ODD_SPLITS = (D_INNER, CONV_CH, C_HEADS, D_HEADS * 2 * D_DH, D_HEADS * 2 * D_DH, D_HEADS * D_DV)
ODD_OUT = D_INNER + D_HEADS * D_DV

kernel_name = 'hybrid_dilated_retention_ssd_diffattn_peer_step'


def rmsnorm(x, g):
    xf = x.astype(jnp.float32)
    y = xf * lax.rsqrt(jnp.mean(xf * xf, axis=-1, keepdims=True) + EPS)
    return (y * g.astype(jnp.float32)).astype(x.dtype)


def rope(x, pos, n_rot, theta):
    half = n_rot // 2
    inv = theta ** (-(jnp.arange(half, dtype=jnp.float32) * 2.0 / n_rot))
    ang = pos.astype(jnp.float32)[:, None] * inv[None, :]
    shape = (1, pos.shape[0]) + (1,) * (x.ndim - 3) + (half,)
    cos = jnp.cos(ang).reshape(shape)
    sin = jnp.sin(ang).reshape(shape)
    xf = x.astype(jnp.float32)
    x1, x2, rest = xf[..., :half], xf[..., half:n_rot], xf[..., n_rot:]
    return jnp.concatenate([x1 * cos - x2 * sin, x1 * sin + x2 * cos, rest], axis=-1).astype(x.dtype)


def split_cols(t, sizes):
    return jnp.split(t, np.cumsum(sizes)[:-1].tolist(), axis=-1)


def dilated_group(q, k, v, q_idx, dil, n_keys):
    idx = q_idx[:, None] - dil * jnp.arange(n_keys, dtype=jnp.int32)[None, :]
    valid = idx >= 0
    idx = jnp.maximum(idx, 0)
    kg = jnp.take(k, idx, axis=1)
    vg = jnp.take(v, idx, axis=1)
    s = jnp.einsum('bthd,btjhd->bthj', q, kg).astype(jnp.float32) * (DH ** -0.5)
    s = jnp.where(valid[None, :, None, :], s, -jnp.inf)
    m = jnp.max(s, axis=-1)
    p = jnp.exp(s - m[..., None])
    den = jnp.sum(p, axis=-1)
    o = jnp.einsum('bthj,btjhd->bthd', p, vg.astype(jnp.float32)) / den[..., None]
    return o, m, den


def dilated_mix(q3, ks, vs, qidxs):
    outs, ms, dens = [], [], []
    for g in range(3):
        o, m, d = dilated_group(q3[:, :, g], ks[g], vs[g], qidxs[g], A_DILATIONS[g],
                                A_WINDOWS[g] // A_DILATIONS[g] + 1)
        outs.append(o)
        ms.append(m)
        dens.append(d)
    m_all = jnp.stack(ms, 0)
    w = jnp.stack(dens, 0) * jnp.exp(m_all - jnp.max(m_all, axis=0))
    o = jnp.sum(w[..., None] * jnp.stack(outs, 0), axis=0) / jnp.sum(w, axis=0)[..., None]
    return o.astype(q3.dtype)


def decay_chunk(q, k, v, log_a, S):
    q, k, v = q.astype(jnp.float32), k.astype(jnp.float32), v.astype(jnp.float32)
    S = S.astype(jnp.float32)
    L = q.shape[1]
    cum = jnp.cumsum(log_a.astype(jnp.float32), axis=1)
    causal = jnp.tril(jnp.ones((L, L), dtype=bool))[None, :, :, None]
    diff = cum[:, :, None, :] - cum[:, None, :, :]
    dec = jnp.where(causal, jnp.exp(jnp.where(causal, diff, 0.0)), 0.0)
    att = jnp.einsum('bihk,bjhk->bijh', q, k) * dec
    o = jnp.einsum('bijh,bjhv->bihv', att, v) + jnp.einsum('bihk,bhkv->bihv', q * jnp.exp(cum)[..., None], S)
    last = cum[:, -1]
    w = jnp.exp(last[:, None, :] - cum)
    S = jnp.exp(last)[:, :, None, None] * S + jnp.einsum('bjhk,bjhv->bhkv', k * w[..., None], v)
    return o, S


def chunked_decay(q, k, v, log_a, S0, chunk):
    B_, L = q.shape[:2]
    n = L // chunk

    def split(t):
        return jnp.moveaxis(t.reshape((B_, n, chunk) + t.shape[2:]), 1, 0)

    def step(S, xs):
        o, S = decay_chunk(xs[0], xs[1], xs[2], xs[3], S)
        return S, o

    S, o = lax.scan(step, S0, (split(q), split(k), split(v), split(log_a)))
    o = jnp.moveaxis(o, 0, 1)
    return o.reshape((B_, L) + o.shape[3:]), S


def causal_conv(u, buf, w, b):
    T = u.shape[1]
    up = jnp.concatenate([buf.astype(u.dtype), u], axis=1)
    out = b
    for i in range(CONV_W):
        out = out + up[:, i:i + T] * w[i]
    return out, up[:, T:]


def diff_attn(q, k, v, lam, q_idx):
    mask = jnp.arange(k.shape[1], dtype=jnp.int32)[None, :] <= q_idx[:, None]
    s = jnp.einsum('bthcd,blhcd->bhctl', q, k).astype(jnp.float32) * (D_DH ** -0.5)
    p = jax.nn.softmax(jnp.where(mask, s, -jnp.inf), axis=-1)
    w = p[:, :, 0] - lam * p[:, :, 1]
    return jnp.einsum('bhtl,blhe->bthe', w, v.astype(jnp.float32))


def peer(h, wq, ka, kb, u_tab, v_tab):
    B_, T, D = h.shape
    n = B_ * T
    nb = -(-n // PEER_BLOCK)
    flat = jnp.pad(h.reshape(n, D), ((0, nb * PEER_BLOCK - n), (0, 0))).reshape(nb, PEER_BLOCK, D)
    half = D_KEY // 2
    nc = PEER_TOPK * PEER_TOPK

    def block(xb):
        q = (xb @ wq).reshape(PEER_BLOCK, PEER_HEADS, D_KEY)
        sa = jnp.einsum('nhd,kd->nhk', q[..., :half], ka).astype(jnp.float32)
        sb = jnp.einsum('nhd,kd->nhk', q[..., half:], kb).astype(jnp.float32)
        va, ia = lax.top_k(sa, PEER_TOPK)
        vb, ib = lax.top_k(sb, PEER_TOPK)
        cand = (va[..., :, None] + vb[..., None, :]).reshape(PEER_BLOCK, PEER_HEADS, nc)
        cidx = (ia[..., :, None] * N_KEYS + ib[..., None, :]).reshape(PEER_BLOCK, PEER_HEADS, nc)
        top, sel = lax.top_k(cand, PEER_TOPK)
        eidx = jnp.take_along_axis(cidx, sel, axis=-1)
        g = jax.nn.softmax(top, axis=-1)
        act = jax.nn.gelu(jnp.einsum('nd,nhkd->nhk', xb, u_tab[eidx]).astype(jnp.float32), approximate=False)
        return jnp.einsum('nhk,nhkd->nd', (g * act).astype(xb.dtype), v_tab[eidx])

    out = lax.map(block, flat).reshape(nb * PEER_BLOCK, D)[:n]
    return out.reshape(B_, T, D)


def even_mix(h, pos, w_in, w_out, qn, kn, a_bufs, ret_state):
    B_, T, _ = h.shape
    qa, ka, va, qr, kr, vr, gr = split_cols(h @ w_in, EVEN_SPLITS)
    hs = (B_, T, 3, A_HEADS, DH)
    qa = rope(rmsnorm(qa.reshape(hs), qn[:, None, :]), pos, ROPE_DIMS, ROPE_THETA)
    ka = rope(rmsnorm(ka.reshape(hs), kn[:, None, :]), pos, ROPE_DIMS, ROPE_THETA)
    va = va.reshape(hs)
    if a_bufs is None:
        ks = [ka[:, :, g] for g in range(3)]
        vs = [va[:, :, g] for g in range(3)]
        nq = T // A_QBLOCK
        qblk = jnp.moveaxis(qa.reshape(B_, nq, A_QBLOCK, 3, A_HEADS, DH), 1, 0)
        starts = jnp.arange(nq, dtype=jnp.int32) * A_QBLOCK

        def blk(args):
            qb, st = args
            qi = st + jnp.arange(A_QBLOCK, dtype=jnp.int32)
            return dilated_mix(qb, ks, vs, (qi, qi, qi))

        oa = jnp.moveaxis(lax.map(blk, (qblk, starts)), 0, 1)
        new_a = []
        for g, win in enumerate(A_WINDOWS):
            keep = min(win, T)
            new_a.append(jnp.stack([ka[:, T - keep:, g], va[:, T - keep:, g]], axis=2))
    else:
        ks = [jnp.concatenate([a_bufs[g][:, :, 0], ka[:, :, g]], axis=1) for g in range(3)]
        vs = [jnp.concatenate([a_bufs[g][:, :, 1], va[:, :, g]], axis=1) for g in range(3)]
        qis = tuple(a_bufs[g].shape[1] + jnp.arange(T, dtype=jnp.int32) for g in range(3))
        oa = dilated_mix(qa, ks, vs, qis)
        new_a = [jnp.stack([ka[:, :, g], va[:, :, g]], axis=2) for g in range(3)]
    oa = oa.reshape(B_, T, A_HEADS * DH)
    qr = rope(qr.reshape(B_, T, B_HEADS, B_DK), pos, B_DK, RET_THETA) * (B_DK ** -0.5)
    kr = rope(kr.reshape(B_, T, B_HEADS, B_DK), pos, B_DK, RET_THETA)
    vr = vr.reshape(B_, T, B_HEADS, B_DV)
    log_g = jnp.log1p(-jnp.exp2(-5.0 - jnp.arange(B_HEADS, dtype=jnp.float32)))
    log_a = jnp.broadcast_to(log_g, (B_, T, B_HEADS))
    if ret_state is None:
        S0 = jnp.zeros((B_, B_HEADS, B_DK, B_DV), jnp.float32)
        o, S = chunked_decay(qr, kr, vr, log_a, S0, RET_CHUNK)
    else:
        o, S = decay_chunk(qr, kr, vr, log_a, ret_state)
    mu = jnp.mean(o, axis=-1, keepdims=True)
    var = jnp.mean(jnp.square(o - mu), axis=-1, keepdims=True)
    o = (o - mu) * lax.rsqrt(var + EPS)
    ob = (o.reshape(B_, T, B_HEADS * B_DV) * jax.nn.silu(gr.astype(jnp.float32))).astype(h.dtype)
    out = jnp.concatenate([oa, ob], axis=-1) @ w_out
    return out, new_a, S


def odd_mix(h, pos, lam_init, w_in, w_out, conv_w, conv_b, dt_bias, a_log, d_skip, ssm_norm,
            d_qn, d_kn, lq1, lk1, lq2, lk2, d_subln, conv_buf, ssm_state, d_past):
    B_, T, _ = h.shape
    z, xbc, dtr, qd, kd, vd = split_cols(h @ w_in, ODD_SPLITS)
    if conv_buf is None:
        conv_buf = jnp.zeros((B_, CONV_W - 1, CONV_CH), xbc.dtype)
    xbc, new_conv = causal_conv(xbc, conv_buf, conv_w, conv_b)
    xbc = jax.nn.silu(xbc)
    xs_, bm, cm = split_cols(xbc, (D_INNER, C_GROUPS * N_STATE, C_GROUPS * N_STATE))
    xs_ = xs_.reshape(B_, T, C_HEADS, C_HEADDIM)
    rep = C_HEADS // C_GROUPS
    bm = jnp.repeat(bm.reshape(B_, T, C_GROUPS, N_STATE), rep, axis=2)
    cm = jnp.repeat(cm.reshape(B_, T, C_GROUPS, N_STATE), rep, axis=2)
    dt = jax.nn.softplus(dtr.astype(jnp.float32) + dt_bias.astype(jnp.float32))
    log_a = dt * (-jnp.exp(a_log.astype(jnp.float32)))
    vin = xs_.astype(jnp.float32) * dt[..., None]
    if ssm_state is None:
        S0 = jnp.zeros((B_, C_HEADS, N_STATE, C_HEADDIM), jnp.float32)
        yo, S = chunked_decay(cm, bm, vin, log_a, S0, SSD_CHUNK)
    else:
        yo, S = decay_chunk(cm, bm, vin, log_a, ssm_state)
    y = yo + d_skip.astype(jnp.float32)[:, None] * xs_.astype(jnp.float32)
    y = rmsnorm(y.reshape(B_, T, D_INNER) * jax.nn.silu(z.astype(jnp.float32)), ssm_norm).astype(h.dtype)
    qd = rope(rmsnorm(qd.reshape(B_, T, D_HEADS, 2, D_DH), d_qn), pos, ROPE_DIMS, ROPE_THETA)
    kd = rope(rmsnorm(kd.reshape(B_, T, D_HEADS, 2, D_DH), d_kn), pos, ROPE_DIMS, ROPE_THETA)
    vd = vd.reshape(B_, T, D_HEADS, D_DV)
    lam = (jnp.exp(jnp.sum(lq1.astype(jnp.float32) * lk1.astype(jnp.float32)))
           - jnp.exp(jnp.sum(lq2.astype(jnp.float32) * lk2.astype(jnp.float32))) + lam_init)
    if d_past is None:
        nq = T // D_QBLOCK
        qblk = jnp.moveaxis(qd.reshape(B_, nq, D_QBLOCK, D_HEADS, 2, D_DH), 1, 0)
        starts = jnp.arange(nq, dtype=jnp.int32) * D_QBLOCK

        def blk(args):
            qb, st = args
            return diff_attn(qb, kd, vd, lam, st + jnp.arange(D_QBLOCK, dtype=jnp.int32))

        od = jnp.moveaxis(lax.map(blk, (qblk, starts)), 0, 1).reshape(B_, T, D_HEADS, D_DV)
    else:
        P = d_past.shape[1]
        k_all = jnp.concatenate([d_past[:, :, 0].reshape(B_, P, D_HEADS, 2, D_DH), kd], axis=1)
        v_all = jnp.concatenate([d_past[:, :, 1], vd], axis=1)
        od = diff_attn(qd, k_all, v_all, lam, P + jnp.arange(T, dtype=jnp.int32))
    new_d = jnp.stack([kd.reshape(B_, T, D_HEADS, 2 * D_DH), vd], axis=2)
    od = (rmsnorm(od, d_subln) * (1.0 - lam_init)).reshape(B_, T, D_HEADS * D_DV).astype(h.dtype)
    out = jnp.concatenate([y, od], axis=-1) @ w_out
    return out, new_conv, S, new_d


def setup_inputs(seed: int = 0) -> dict:
    key = jax.random.key(seed)
    ks = iter(jax.random.split(key, 48))
    f32 = jnp.float32

    def nrm(shape, scale=1.0):
        return jax.random.normal(next(ks), shape, f32) * scale

    def gain(shape):
        return 1.0 + 0.02 * jax.random.normal(next(ks), shape, f32)

    n_pages = PAST_LEN // PAGE_SIZE
    n_pool = (DEC_BATCH * n_pages * 5 + 3) // 4
    page_table = jax.random.permutation(next(ks), n_pool)[:DEC_BATCH * n_pages].reshape(DEC_BATCH, n_pages).astype(jnp.int32)
    dt0 = jnp.exp(jax.random.uniform(next(ks), (N_ODD, C_HEADS), f32, math.log(1e-3), math.log(1e-1)))
    dt_bias = dt0 + jnp.log(-jnp.expm1(-dt0))
    a_log = jnp.log(jax.random.uniform(next(ks), (N_ODD, C_HEADS), f32, 1.0, 16.0))
    return {
        'x_prompt': nrm((BATCH, SEQ, D_MODEL)),
        'x_sample': nrm((DEC_BATCH, DEC_SEQ, D_MODEL)),
        'cache_a1': nrm((N_EVEN, DEC_BATCH, min(A_WINDOWS[0], PAST_LEN), 2, A_HEADS, DH)),
        'cache_a2': nrm((N_EVEN, DEC_BATCH, min(A_WINDOWS[1], PAST_LEN), 2, A_HEADS, DH)),
        'cache_a3': nrm((N_EVEN, DEC_BATCH, min(A_WINDOWS[2], PAST_LEN), 2, A_HEADS, DH)),
        'state_ret': nrm((N_EVEN, DEC_BATCH, B_HEADS, B_DK, B_DV), 0.1),
        'state_ssm': nrm((N_ODD, DEC_BATCH, C_HEADS, N_STATE, C_HEADDIM), 0.1),
        'state_conv': nrm((N_ODD, DEC_BATCH, CONV_W - 1, CONV_CH)),
        'cache_d': nrm((N_ODD, n_pool, PAGE_SIZE, 2, D_HEADS, 2 * D_DH)),
        'page_table': page_table,
        'norm_mix': gain((DEPTH, D_MODEL)),
        'norm_ffn': gain((DEPTH, D_MODEL)),
        'w_in_even': nrm((N_EVEN, D_MODEL, sum(EVEN_SPLITS)), D_MODEL ** -0.5),
        'w_out_even': nrm((N_EVEN, EVEN_OUT, D_MODEL), 0.5 * EVEN_OUT ** -0.5),
        'a_qnorm': gain((N_EVEN, 3, DH)),
        'a_knorm': gain((N_EVEN, 3, DH)),
        'w_in_odd': nrm((N_ODD, D_MODEL, sum(ODD_SPLITS)), D_MODEL ** -0.5),
        'w_out_odd': nrm((N_ODD, ODD_OUT, D_MODEL), 0.5 * ODD_OUT ** -0.5),
        'conv_w': nrm((N_ODD, CONV_W, CONV_CH), CONV_W ** -0.5),
        'conv_b': nrm((N_ODD, CONV_CH), 0.01),
        'dt_bias': dt_bias,
        'a_log': a_log,
        'd_skip': 1.0 + nrm((N_ODD, C_HEADS), 0.1),
        'ssm_norm': gain((N_ODD, D_INNER)),
        'd_qnorm': gain((N_ODD, D_DH)),
        'd_knorm': gain((N_ODD, D_DH)),
        'lam_q1': nrm((N_ODD, D_DH), 0.1),
        'lam_k1': nrm((N_ODD, D_DH), 0.1),
        'lam_q2': nrm((N_ODD, D_DH), 0.1),
        'lam_k2': nrm((N_ODD, D_DH), 0.1),
        'd_subln': gain((N_ODD, D_DV)),
        'peer_wq': nrm((DEPTH, D_MODEL, PEER_HEADS * D_KEY), D_MODEL ** -0.5),
        'peer_ka': nrm((DEPTH, N_KEYS, D_KEY // 2), (D_KEY // 2) ** -0.5),
        'peer_kb': nrm((DEPTH, N_KEYS, D_KEY // 2), (D_KEY // 2) ** -0.5),
        'peer_u': nrm((DEPTH, N_EXPERTS, D_MODEL), D_MODEL ** -0.5),
        'peer_v': nrm((DEPTH, N_EXPERTS, D_MODEL), 0.5 * PEER_HEADS ** -0.5),
    }


def reference(x_prompt, x_sample, cache_a1, cache_a2, cache_a3, state_ret, state_ssm, state_conv,
              cache_d, page_table, norm_mix, norm_ffn, w_in_even, w_out_even, a_qnorm, a_knorm,
              w_in_odd, w_out_odd, conv_w, conv_b, dt_bias, a_log, d_skip, ssm_norm,
              d_qnorm, d_knorm, lam_q1, lam_k1, lam_q2, lam_k2, d_subln,
              peer_wq, peer_ka, peer_kb, peer_u, peer_v):
    pos_p = jnp.arange(x_prompt.shape[1], dtype=jnp.int32)
    pos_s = PAST_LEN + jnp.arange(x_sample.shape[1], dtype=jnp.int32)
    xp, xs = x_prompt, x_sample
    a_p = ([], [], [])
    a_s = ([], [], [])
    ret_p, ret_s, ssm_p, ssm_s, conv_p, conv_s, d_p, d_s = [], [], [], [], [], [], [], []
    for l in range(DEPTH):
        hp = rmsnorm(xp, norm_mix[l])
        hs = rmsnorm(xs, norm_mix[l])
        if l % 2 == 0:
            e = l // 2
            wargs = (w_in_even[e], w_out_even[e], a_qnorm[e], a_knorm[e])
            op, nap, rp = even_mix(hp, pos_p, *wargs, None, None)
            osm, nas, rs = even_mix(hs, pos_s, *wargs, (cache_a1[e], cache_a2[e], cache_a3[e]), state_ret[e])
            for g in range(3):
                a_p[g].append(nap[g])
                a_s[g].append(nas[g])
            ret_p.append(rp)
            ret_s.append(rs)
        else:
            o = l // 2
            lam_init = 0.8 - 0.6 * math.exp(-0.3 * l)
            wargs = (w_in_odd[o], w_out_odd[o], conv_w[o], conv_b[o], dt_bias[o], a_log[o], d_skip[o],
                     ssm_norm[o], d_qnorm[o], d_knorm[o], lam_q1[o], lam_k1[o], lam_q2[o], lam_k2[o], d_subln[o])
            past = cache_d[o][page_table]
            past = past.reshape((page_table.shape[0], -1) + past.shape[3:])
            op, cp, sp, dp = odd_mix(hp, pos_p, lam_init, *wargs, None, None, None)
            osm, cs, ss, ds = odd_mix(hs, pos_s, lam_init, *wargs, state_conv[o], state_ssm[o], past)
            conv_p.append(cp)
            conv_s.append(cs)
            ssm_p.append(sp)
            ssm_s.append(ss)
            d_p.append(dp)
            d_s.append(ds)
        xp = xp + op
        xs = xs + osm
        pargs = (peer_wq[l], peer_ka[l], peer_kb[l], peer_u[l], peer_v[l])
        xp = xp + peer(rmsnorm(xp, norm_ffn[l]), *pargs)
        xs = xs + peer(rmsnorm(xs, norm_ffn[l]), *pargs)
    return (xp, xs,
            jnp.stack(a_p[0], 0), jnp.stack(a_s[0], 0),
            jnp.stack(a_p[1], 0), jnp.stack(a_s[1], 0),
            jnp.stack(a_p[2], 0), jnp.stack(a_s[2], 0),
            jnp.stack(ret_p, 0), jnp.stack(ret_s, 0),
            jnp.stack(ssm_p, 0), jnp.stack(ssm_s, 0),
            jnp.stack(conv_p, 0), jnp.stack(conv_s, 0),
            jnp.stack(d_p, 0), jnp.stack(d_s, 0))
```

```python
import functools
import math

import jax
import jax.numpy as jnp
from jax import lax
from jax.experimental import pallas as pl
from jax.experimental.pallas import tpu as pltpu

F32 = jnp.float32
BF16 = jnp.bfloat16
NEG = -1e30

EPS = 1e-6
DH = 64
ROPE_DIMS = DH // 4
ROPE_THETA = 500000.0
A_HEADS = 8
A_WINDOWS = (128, 512, 2048)
A_DILATIONS = (1, 4, 16)
B_HEADS = 4
B_DK = 64
B_DV = 128
RET_THETA = 10000.0
D_INNER = 1024
C_HEADDIM = 64
C_HEADS = D_INNER // C_HEADDIM
C_GROUPS = 2
N_STATE = 128
CONV_W = 4
CONV_CH = D_INNER + 2 * C_GROUPS * N_STATE
D_HEADS = 4
D_DH = 64
D_DV = 2 * D_DH
PEER_HEADS = 8
N_KEYS = 128
PEER_TOPK = 16
D_KEY = 256
PAGE_SIZE = 128

CHUNK = 128
SAMPLE_PAD = 16
TOKEN_TILE = 512
ATTN_TILE = 128
EXPERT_TILE = 512
VMEM_LIMIT = 48 * 1024 * 1024

_NT = (((1,), (1,)), ((), ()))
_TN = (((0,), (0,)), ((), ()))


def _params(*sem):
    return pltpu.CompilerParams(dimension_semantics=sem, vmem_limit_bytes=VMEM_LIMIT)


def _bdot(a, b):
    return jnp.dot(a.astype(BF16), b.astype(BF16), preferred_element_type=F32)


def _norm_matmul_kernel(x_ref, g_ref, w_ref, o_ref, h_ref):
    @pl.when(pl.program_id(1) == 0)
    def _():
        x = x_ref[...]
        ms = jnp.mean(x * x, axis=-1, keepdims=True)
        h_ref[...] = (x * lax.rsqrt(ms + EPS) * g_ref[...]).astype(BF16)

    o_ref[...] = jnp.dot(h_ref[...], w_ref[...], preferred_element_type=F32)


def norm_matmul(x, g, w, tn):
    n, d = x.shape
    m = w.shape[1]
    tm = TOKEN_TILE
    return pl.pallas_call(
        _norm_matmul_kernel,
        grid=(n // tm, m // tn),
        in_specs=[
            pl.BlockSpec((tm, d), lambda i, j: (i, 0)),
            pl.BlockSpec((1, d), lambda i, j: (0, 0)),
            pl.BlockSpec((d, tn), lambda i, j: (0, j)),
        ],
        out_specs=pl.BlockSpec((tm, tn), lambda i, j: (i, j)),
        out_shape=jax.ShapeDtypeStruct((n, m), F32),
        scratch_shapes=[pltpu.VMEM((tm, d), BF16)],
        compiler_params=_params("parallel", "arbitrary"),
        name="norm_matmul",
    )(x, g.reshape(1, d), w)


def _matmul_res_kernel(x_ref, w_ref, r_ref, o_ref):
    o_ref[...] = r_ref[...] + jnp.dot(x_ref[...], w_ref[...], preferred_element_type=F32)


def matmul_res(x, w, res):
    n, k = x.shape
    d = w.shape[1]
    tm = TOKEN_TILE
    return pl.pallas_call(
        _matmul_res_kernel,
        grid=(n // tm,),
        in_specs=[
            pl.BlockSpec((tm, k), lambda i: (i, 0)),
            pl.BlockSpec((k, d), lambda i: (0, 0)),
            pl.BlockSpec((tm, d), lambda i: (i, 0)),
        ],
        out_specs=pl.BlockSpec((tm, d), lambda i: (i, 0)),
        out_shape=jax.ShapeDtypeStruct((n, d), F32),
        compiler_params=_params("parallel"),
        name="matmul_res",
    )(x, w, res)


def _flash_update(carry, s, v):
    m, l, acc = carry
    m_new = jnp.maximum(m, jnp.max(s, axis=-1, keepdims=True))
    alpha = jnp.exp(m - m_new)
    p = jnp.exp(s - m_new)
    l = alpha * l + jnp.sum(p, axis=-1, keepdims=True)
    acc = alpha * acc + jnp.dot(p.astype(BF16), v, preferred_element_type=F32)
    return m_new, l, acc


def _flash_update_rows(carry, s_list, v_list):
    m, l, acc = carry
    m_new = m
    for s in s_list:
        m_new = jnp.maximum(m_new, s)
    alpha = jnp.exp(m - m_new)
    l = alpha * l
    acc = alpha * acc
    for s, v in zip(s_list, v_list):
        p = jnp.exp(s - m_new)
        l = l + p
        acc = acc + p * v
    return m_new, l, acc


def _flash_init(t, e):
    return (jnp.full((t, 1), NEG, F32), jnp.zeros((t, 1), F32), jnp.zeros((t, e), F32))


def _dilated_prompt_kernel(q_ref, k_ref, v_ref, o_ref):
    tq = ATTN_TILE
    i = pl.program_id(1)
    rows = lax.broadcasted_iota(jnp.int32, (tq, tq), 0)
    cols = lax.broadcasted_iota(jnp.int32, (tq, tq), 1)
    rel = rows - cols
    for h in range(A_HEADS):
        carry = _flash_init(tq, DH)
        for g in range(3):
            c0 = g * A_HEADS * DH + h * DH
            win, dil = A_WINDOWS[g], A_DILATIONS[g]
            q = q_ref[0, :, c0:c0 + DH]

            def body(j, carry, c0=c0, win=win, dil=dil, q=q):
                ks = pl.multiple_of(j * tq, tq)
                kb = k_ref[0, pl.ds(ks, tq), c0:c0 + DH]
                vb = v_ref[0, pl.ds(ks, tq), c0:c0 + DH]
                s = lax.dot_general(q, kb, _NT, preferred_element_type=F32)
                delta = (i - j) * tq + rel
                valid = (delta >= 0) & (delta <= win) & ((delta & (dil - 1)) == 0)
                return _flash_update(carry, jnp.where(valid, s, NEG), vb)

            carry = lax.fori_loop(jnp.maximum(i - win // tq, 0), i + 1, body, carry)
        _, l, acc = carry
        o_ref[0, :, h * DH:(h + 1) * DH] = acc / l


def dilated_prompt(q, k, v):
    b, t, c = q.shape
    tq = ATTN_TILE
    return pl.pallas_call(
        _dilated_prompt_kernel,
        grid=(b, t // tq),
        in_specs=[
            pl.BlockSpec((1, tq, c), lambda bi, i: (bi, i, 0)),
            pl.BlockSpec((1, t, c), lambda bi, i: (bi, 0, 0)),
            pl.BlockSpec((1, t, c), lambda bi, i: (bi, 0, 0)),
        ],
        out_specs=pl.BlockSpec((1, tq, A_HEADS * DH), lambda bi, i: (bi, i, 0)),
        out_shape=jax.ShapeDtypeStruct((b, t, A_HEADS * DH), F32),
        compiler_params=_params("parallel", "arbitrary"),
        name="dilated_prompt",
    )(q, k, v)


def _dilated_sample_kernel(q_ref, kn_ref, vn_ref, a1_ref, a2_ref, a3_ref, o_ref):
    tp = SAMPLE_PAD
    hw = A_HEADS * DH
    n_new = kn_ref.shape[1]

    def row_scores(qf, g, h, shift):
        kn = kn_ref[0, shift, :, g * hw + h * DH: g * hw + (h + 1) * DH]
        kn = kn.astype(BF16).astype(F32)
        return jnp.sum(qf * kn, axis=-1, keepdims=True)

    def new_values(g, h, shift):
        return vn_ref[0, shift, :, g * hw + h * DH: g * hw + (h + 1) * DH]

    t1 = lax.broadcasted_iota(jnp.int32, (tp, 1), 0)
    for h in range(A_HEADS):
        carry = _flash_init(tp, DH)
        q = q_ref[0, :, h * DH:(h + 1) * DH]
        kc = a1_ref[0, 0, :, h * DH:(h + 1) * DH].astype(BF16)
        vc = a1_ref[0, 0, :, hw + h * DH: hw + (h + 1) * DH].astype(BF16)
        s = lax.dot_general(q, kc, _NT, preferred_element_type=F32)
        r = lax.broadcasted_iota(jnp.int32, s.shape, 1)
        t = lax.broadcasted_iota(jnp.int32, s.shape, 0)
        carry = _flash_update(carry, jnp.where(r >= t, s, NEG), vc)
        qf = q.astype(F32)
        s_new = [jnp.where(t1 >= d, row_scores(qf, 0, h, d), NEG) for d in range(n_new)]
        carry = _flash_update_rows(carry, s_new, [new_values(0, h, d) for d in range(n_new)])
        q = q_ref[0, :, hw + h * DH: hw + (h + 1) * DH]
        kc = a2_ref[0, 0, :, h * DH:(h + 1) * DH].astype(BF16)
        vc = a2_ref[0, 0, :, hw + h * DH: hw + (h + 1) * DH].astype(BF16)
        s = lax.dot_general(q, kc, _NT, preferred_element_type=F32)
        r = lax.broadcasted_iota(jnp.int32, s.shape, 1)
        t = lax.broadcasted_iota(jnp.int32, s.shape, 0)
        carry = _flash_update(carry, jnp.where(((r - t) & 3) == 0, s, NEG), vc)
        carry = _flash_update_rows(carry, [row_scores(q.astype(F32), 1, h, 0)], [new_values(1, h, 0)])
        q = q_ref[0, :, 2 * hw + h * DH: 2 * hw + (h + 1) * DH]
        for r4 in range(4):
            base = r4 * 2 * hw
            kc = a3_ref[0, 0, :, base + h * DH: base + (h + 1) * DH].astype(BF16)
            vc = a3_ref[0, 0, :, base + hw + h * DH: base + hw + (h + 1) * DH].astype(BF16)
            s = lax.dot_general(q, kc, _NT, preferred_element_type=F32)
            t = lax.broadcasted_iota(jnp.int32, s.shape, 0)
            carry = _flash_update(carry, jnp.where(t == r4, s, NEG), vc)
        carry = _flash_update_rows(carry, [row_scores(q.astype(F32), 2, h, 0)], [new_values(2, h, 0)])
        _, l, acc = carry
        o_ref[0, :, h * DH:(h + 1) * DH] = acc / l


def dilated_sample(q, kn, vn, a1, a2, a3, e):
    b, tp, c = q.shape
    hw = A_HEADS * DH
    a1 = a1.reshape(a1.shape[0], b, a1.shape[2], 2 * hw)
    a2 = a2.reshape(a2.shape[0], b, a2.shape[2], 2 * hw)
    dil = A_DILATIONS[2]
    a3 = a3.reshape(a3.shape[0], b, a3.shape[2] // dil, dil * 2 * hw)
    n_res = kn.shape[1]
    return pl.pallas_call(
        _dilated_sample_kernel,
        grid=(b,),
        in_specs=[
            pl.BlockSpec((1, tp, c), lambda i: (i, 0, 0)),
            pl.BlockSpec((1, n_res, tp, c), lambda i: (i, 0, 0, 0)),
            pl.BlockSpec((1, n_res, tp, c), lambda i: (i, 0, 0, 0)),
            pl.BlockSpec((1, 1, a1.shape[2], 2 * hw), lambda i: (e, i, 0, 0)),
            pl.BlockSpec((1, 1, a2.shape[2], 2 * hw), lambda i: (e, i, 0, 0)),
            pl.BlockSpec((1, 1, a3.shape[2], n_res * 2 * hw), lambda i: (e, i, 0, 0)),
        ],
        out_specs=pl.BlockSpec((1, tp, hw), lambda i: (i, 0, 0)),
        out_shape=jax.ShapeDtypeStruct((b, tp, hw), F32),
        compiler_params=_params("parallel"),
        name="dilated_sample",
    )(q, kn, vn, a1, a2, a3)


def _decay_kernel(q_ref, k_ref, v_ref, la_ref, lat_ref, s0_ref, o_ref, sout_ref, s_scr,
                  *, heads, groups, dk, dv):
    c = pl.program_id(1)
    ln = q_ref.shape[1]

    @pl.when(c == 0)
    def _():
        s_scr[...] = s0_ref[0]

    r = lax.broadcasted_iota(jnp.int32, (ln, ln), 0)
    cidx = lax.broadcasted_iota(jnp.int32, (ln, ln), 1)
    tril = r >= cidx
    cum = jnp.dot(tril.astype(F32), la_ref[0], precision=lax.Precision.HIGHEST,
                  preferred_element_type=F32)
    cum_t = jnp.dot(lat_ref[0], (r <= cidx).astype(F32), precision=lax.Precision.HIGHEST,
                    preferred_element_type=F32)
    rep = heads // groups
    for g in range(groups):
        qg = q_ref[0, :, g * dk:(g + 1) * dk]
        kg = k_ref[0, :, g * dk:(g + 1) * dk]
        qk = lax.dot_general(qg.astype(BF16), kg.astype(BF16), _NT, preferred_element_type=F32)
        for hh in range(rep):
            h = g * rep + hh
            cc = cum[:, h:h + 1]
            diff = cc - cum_t[h:h + 1, :]
            dec = jnp.where(tril, jnp.exp(jnp.where(tril, diff, 0.0)), 0.0)
            vh = v_ref[0, :, h * dv:(h + 1) * dv].astype(BF16)
            s_old = s_scr[h]
            o = _bdot(qk * dec, vh) + _bdot(qg * jnp.exp(cc), s_old)
            o_ref[0, :, h * dv:(h + 1) * dv] = o
            last = cum[ln - 1:ln, h:h + 1]
            kw = (kg * jnp.exp(last - cc)).astype(BF16)
            s_scr[h] = jnp.exp(last) * s_old + lax.dot_general(kw, vh, _TN, preferred_element_type=F32)

    @pl.when(c == pl.num_programs(1) - 1)
    def _():
        sout_ref[0] = s_scr[...]


def decay_scan(q, k, v, log_a, s0, chunk):
    b, t, gk = q.shape
    _, heads, dk, dv = s0.shape
    groups = gk // dk
    kern = functools.partial(_decay_kernel, heads=heads, groups=groups, dk=dk, dv=dv)
    lat = jnp.swapaxes(log_a, 1, 2)
    return pl.pallas_call(
        kern,
        grid=(b, t // chunk),
        in_specs=[
            pl.BlockSpec((1, chunk, gk), lambda bi, c: (bi, c, 0)),
            pl.BlockSpec((1, chunk, gk), lambda bi, c: (bi, c, 0)),
            pl.BlockSpec((1, chunk, heads * dv), lambda bi, c: (bi, c, 0)),
            pl.BlockSpec((1, chunk, heads), lambda bi, c: (bi, c, 0)),
            pl.BlockSpec((1, heads, chunk), lambda bi, c: (bi, 0, c)),
            pl.BlockSpec((1, heads, dk, dv), lambda bi, c: (bi, 0, 0, 0)),
        ],
        out_specs=[
            pl.BlockSpec((1, chunk, heads * dv), lambda bi, c: (bi, c, 0)),
            pl.BlockSpec((1, heads, dk, dv), lambda bi, c: (bi, 0, 0, 0)),
        ],
        out_shape=[
            jax.ShapeDtypeStruct((b, t, heads * dv), F32),
            jax.ShapeDtypeStruct((b, heads, dk, dv), F32),
        ],
        scratch_shapes=[pltpu.VMEM((heads, dk, dv), F32)],
        compiler_params=_params("parallel", "arbitrary"),
        name="decay_scan",
    )(q, k, v, log_a, lat, s0)


def _diff_prompt_kernel(lam_ref, q_ref, k_ref, v_ref, o_ref):
    tq = ATTN_TILE
    i = pl.program_id(1)
    lam = lam_ref[0]
    rows = lax.broadcasted_iota(jnp.int32, (tq, tq), 0)
    cols = lax.broadcasted_iota(jnp.int32, (tq, tq), 1)
    rel = rows - cols
    for h in range(D_HEADS):
        v0 = h * D_DV
        outs = []
        for c in range(2):
            c0 = h * 2 * D_DH + c * D_DH
            q = q_ref[0, :, c0:c0 + D_DH]

            def body(j, carry, c0=c0, v0=v0, q=q):
                ks = pl.multiple_of(j * tq, tq)
                kb = k_ref[0, pl.ds(ks, tq), c0:c0 + D_DH]
                vb = v_ref[0, pl.ds(ks, tq), v0:v0 + D_DV]
                s = lax.dot_general(q, kb, _NT, preferred_element_type=F32)
                valid = (i - j) * tq + rel >= 0
                return _flash_update(carry, jnp.where(valid, s, NEG), vb)

            _, l, acc = lax.fori_loop(0, i + 1, body, _flash_init(tq, D_DV))
            outs.append(acc / l)
        o_ref[0, :, v0:v0 + D_DV] = outs[0] - lam * outs[1]


def diff_prompt(lam, q, k, v):
    b, t, c = q.shape
    tq = ATTN_TILE
    return pl.pallas_call(
        _diff_prompt_kernel,
        grid=(b, t // tq),
        in_specs=[
            pl.BlockSpec(memory_space=pltpu.SMEM),
            pl.BlockSpec((1, tq, c), lambda bi, i: (bi, i, 0)),
            pl.BlockSpec((1, t, c), lambda bi, i: (bi, 0, 0)),
            pl.BlockSpec((1, t, c), lambda bi, i: (bi, 0, 0)),
        ],
        out_specs=pl.BlockSpec((1, tq, c), lambda bi, i: (bi, i, 0)),
        out_shape=jax.ShapeDtypeStruct((b, t, c), F32),
        compiler_params=_params("parallel", "arbitrary"),
        name="diff_prompt",
    )(lam.reshape(1), q, k, v)


def _diff_sample_kernel(pt_ref, lam_ref, q_ref, kn_ref, vn_ref, page_ref, o_ref, m_scr, l_scr, acc_scr):
    del pt_ref
    tp = SAMPLE_PAD
    p = pl.program_id(1)
    hw = D_HEADS * D_DV
    n_new = kn_ref.shape[1]

    @pl.when(p == 0)
    def _():
        m_scr[...] = jnp.full(m_scr.shape, NEG, F32)
        l_scr[...] = jnp.zeros(l_scr.shape, F32)
        acc_scr[...] = jnp.zeros(acc_scr.shape, F32)

    for h in range(D_HEADS):
        v0 = h * D_DV
        vp = page_ref[0, 0, :, hw + v0: hw + v0 + D_DV].astype(BF16)
        for c in range(2):
            idx = 2 * h + c
            c0 = h * 2 * D_DH + c * D_DH
            q = q_ref[0, :, c0:c0 + D_DH]
            kp = page_ref[0, 0, :, c0:c0 + D_DH].astype(BF16)
            s = lax.dot_general(q, kp, _NT, preferred_element_type=F32)
            m, l, acc = _flash_update((m_scr[idx], l_scr[idx], acc_scr[idx]), s, vp)
            m_scr[idx] = m
            l_scr[idx] = l
            acc_scr[idx] = acc

    @pl.when(p == pl.num_programs(1) - 1)
    def _():
        lam = lam_ref[0]
        t1 = lax.broadcasted_iota(jnp.int32, (tp, 1), 0)
        for h in range(D_HEADS):
            v0 = h * D_DV
            outs = []
            for c in range(2):
                idx = 2 * h + c
                c0 = h * 2 * D_DH + c * D_DH
                qf = q_ref[0, :, c0:c0 + D_DH].astype(F32)
                s_new, v_new = [], []
                for d in range(n_new):
                    kn = kn_ref[0, d, :, c0:c0 + D_DH].astype(BF16).astype(F32)
                    s_new.append(jnp.where(t1 >= d, jnp.sum(qf * kn, axis=-1, keepdims=True), NEG))
                    v_new.append(vn_ref[0, d, :, v0:v0 + D_DV])
                _, l, acc = _flash_update_rows((m_scr[idx], l_scr[idx], acc_scr[idx]), s_new, v_new)
                outs.append(acc / l)
            o_ref[0, :, v0:v0 + D_DV] = outs[0] - lam * outs[1]


def diff_sample(page_table, lam, q, kn, vn, cache_d, o_idx):
    b, tp, c = q.shape
    n_pages = page_table.shape[1]
    n_new = kn.shape[1]
    pages = cache_d.reshape(cache_d.shape[0], cache_d.shape[1], PAGE_SIZE, 2 * c)
    grid_spec = pltpu.PrefetchScalarGridSpec(
        num_scalar_prefetch=1,
        grid=(b, n_pages),
        in_specs=[
            pl.BlockSpec(memory_space=pltpu.SMEM),
            pl.BlockSpec((1, tp, c), lambda bi, p, pt: (bi, 0, 0)),
            pl.BlockSpec((1, n_new, tp, c), lambda bi, p, pt: (bi, 0, 0, 0)),
            pl.BlockSpec((1, n_new, tp, c), lambda bi, p, pt: (bi, 0, 0, 0)),
            pl.BlockSpec((1, 1, PAGE_SIZE, 2 * c), lambda bi, p, pt: (o_idx, pt[bi * n_pages + p], 0, 0)),
        ],
        out_specs=pl.BlockSpec((1, tp, c), lambda bi, p, pt: (bi, 0, 0)),
        scratch_shapes=[
            pltpu.VMEM((2 * D_HEADS, tp, 1), F32),
            pltpu.VMEM((2 * D_HEADS, tp, 1), F32),
            pltpu.VMEM((2 * D_HEADS, tp, D_DV), F32),
        ],
    )
    return pl.pallas_call(
        _diff_sample_kernel,
        grid_spec=grid_spec,
        out_shape=jax.ShapeDtypeStruct((b, tp, c), F32),
        compiler_params=_params("parallel", "arbitrary"),
        name="diff_sample",
    )(page_table.reshape(-1), lam.reshape(1), q, kn, vn, pages)


def _peer_score_kernel(x_ref, g_ref, wq_ref, ka_ref, kb_ref, h_ref, sa_ref, sb_ref):
    x = x_ref[...]
    ms = jnp.mean(x * x, axis=-1, keepdims=True)
    hb = (x * lax.rsqrt(ms + EPS) * g_ref[...]).astype(BF16)
    h_ref[...] = hb
    half = D_KEY // 2
    for hd in range(PEER_HEADS):
        q = jnp.dot(hb, wq_ref[:, hd * D_KEY:(hd + 1) * D_KEY], preferred_element_type=F32)
        qa = q[:, :half].astype(BF16)
        qb = q[:, half:].astype(BF16)
        sa_ref[hd] = lax.dot_general(ka_ref[...], qa, _NT, preferred_element_type=F32)
        sb_ref[hd] = lax.dot_general(kb_ref[...], qb, _NT, preferred_element_type=F32)


def peer_scores(x, g, wq, ka, kb):
    n, d = x.shape
    tm = TOKEN_TILE
    half = D_KEY // 2
    return pl.pallas_call(
        _peer_score_kernel,
        grid=(n // tm,),
        in_specs=[
            pl.BlockSpec((tm, d), lambda i: (i, 0)),
            pl.BlockSpec((1, d), lambda i: (0, 0)),
            pl.BlockSpec((d, PEER_HEADS * D_KEY), lambda i: (0, 0)),
            pl.BlockSpec((N_KEYS, half), lambda i: (0, 0)),
            pl.BlockSpec((N_KEYS, half), lambda i: (0, 0)),
        ],
        out_specs=[
            pl.BlockSpec((tm, d), lambda i: (i, 0)),
            pl.BlockSpec((PEER_HEADS, N_KEYS, tm), lambda i: (0, 0, i)),
            pl.BlockSpec((PEER_HEADS, N_KEYS, tm), lambda i: (0, 0, i)),
        ],
        out_shape=[
            jax.ShapeDtypeStruct((n, d), BF16),
            jax.ShapeDtypeStruct((PEER_HEADS, N_KEYS, n), F32),
            jax.ShapeDtypeStruct((PEER_HEADS, N_KEYS, n), F32),
        ],
        compiler_params=_params("parallel"),
        name="peer_scores",
    )(x, g.reshape(1, d), wq, ka, kb)


_PAIRS = [(a, b) for a in range(PEER_TOPK) for b in range(PEER_TOPK) if (a + 1) * (b + 1) <= PEER_TOPK]
_N_CAND = -(-len(_PAIRS) // 8) * 8


def _extract_top(work_ref, out_ref):
    def body(k, carry):
        x = work_ref[...]
        m = jnp.max(x, axis=0, keepdims=True)
        out_ref[pl.ds(k, 1), :] = m
        work_ref[...] = jnp.where(x >= m, -jnp.inf, x)
        return carry

    lax.fori_loop(0, PEER_TOPK, body, 0)


def _peer_topk_kernel(sa_ref, sb_ref, ea_ref, eb_ref, tau_ref, work_ref, va_ref, vb_ref, cand_ref, top_ref):
    sa = sa_ref[0]
    sb = sb_ref[0]
    work_ref[...] = sa
    _extract_top(work_ref, va_ref)
    work_ref[...] = sb
    _extract_top(work_ref, vb_ref)
    cand_ref[...] = jnp.full(cand_ref.shape, -jnp.inf, F32)
    for row, (a, b) in enumerate(_PAIRS):
        cand_ref[pl.ds(row, 1), :] = va_ref[pl.ds(a, 1), :] + vb_ref[pl.ds(b, 1), :]
    _extract_top(cand_ref, top_ref)
    top = top_ref[...]
    z = jnp.sum(jnp.exp(top - top[0:1, :]), axis=0, keepdims=True)
    tau_ref[0] = top[PEER_TOPK - 1:PEER_TOPK, :]
    ea_ref[0] = jnp.exp(sa - va_ref[0:1, :]) / z
    eb_ref[0] = jnp.exp(sb - vb_ref[0:1, :])


def peer_topk(sa, sb):
    heads, nk, n = sa.shape
    tk = TOKEN_TILE
    spec = pl.BlockSpec((1, nk, tk), lambda h, i: (h, 0, i))
    return pl.pallas_call(
        _peer_topk_kernel,
        grid=(heads, n // tk),
        in_specs=[spec, spec],
        out_specs=[spec, spec, pl.BlockSpec((1, 1, tk), lambda h, i: (h, 0, i))],
        out_shape=[
            jax.ShapeDtypeStruct((heads, nk, n), F32),
            jax.ShapeDtypeStruct((heads, nk, n), F32),
            jax.ShapeDtypeStruct((heads, 1, n), F32),
        ],
        scratch_shapes=[
            pltpu.VMEM((nk, tk), F32),
            pltpu.VMEM((PEER_TOPK, tk), F32),
            pltpu.VMEM((PEER_TOPK, tk), F32),
            pltpu.VMEM((_N_CAND, tk), F32),
            pltpu.VMEM((PEER_TOPK, tk), F32),
        ],
        compiler_params=_params("parallel", "parallel"),
        name="peer_topk",
    )(sa, sb)


def _peer_dense_kernel(ht_ref, u_ref, vt_ref, sa_ref, sb_ref, ea_ref, eb_ref, tau_ref, o_ref):
    e = pl.program_id(1)

    @pl.when(e == 0)
    def _():
        o_ref[...] = jnp.zeros(o_ref.shape, F32)

    a = jnp.dot(u_ref[...], ht_ref[...], preferred_element_type=F32)
    act = 0.5 * a * (1.0 + lax.erf(a * math.sqrt(0.5)))
    n_sub = EXPERT_TILE // N_KEYS
    blocks = []
    for ii in range(n_sub):
        i = e * n_sub + ii
        w = jnp.zeros((N_KEYS, ht_ref.shape[1]), F32)
        for hd in range(PEER_HEADS):
            s = sb_ref[hd] + sa_ref[hd, pl.ds(i, 1), :]
            g = eb_ref[hd] * ea_ref[hd, pl.ds(i, 1), :]
            w = w + jnp.where(s >= tau_ref[hd], g, 0.0)
        blocks.append(w)
    w = jnp.concatenate(blocks, axis=0)
    o_ref[...] += jnp.dot(vt_ref[...], (w * act).astype(BF16), preferred_element_type=F32)


def peer_dense(ht, u, vt, sa, sb, ea, eb, tau):
    d, n = ht.shape
    n_exp = u.shape[0]
    tm, te = TOKEN_TILE, EXPERT_TILE
    score = pl.BlockSpec((PEER_HEADS, N_KEYS, tm), lambda i, e: (0, 0, i))
    return pl.pallas_call(
        _peer_dense_kernel,
        grid=(n // tm, n_exp // te),
        in_specs=[
            pl.BlockSpec((d, tm), lambda i, e: (0, i)),
            pl.BlockSpec((te, d), lambda i, e: (e, 0)),
            pl.BlockSpec((d, te), lambda i, e: (0, e)),
            score, score, score, score,
            pl.BlockSpec((PEER_HEADS, 1, tm), lambda i, e: (0, 0, i)),
        ],
        out_specs=pl.BlockSpec((d, tm), lambda i, e: (0, i)),
        out_shape=jax.ShapeDtypeStruct((d, n), F32),
        compiler_params=_params("parallel", "arbitrary"),
        name="peer_dense",
    )(ht, u, vt, sa, sb, ea, eb, tau)


def peer(x, g, wq, ka, kb, u, vt):
    hb, sa, sb = peer_scores(x, g, wq, ka, kb)
    ea, eb, tau = peer_topk(sa, sb)
    out_t = peer_dense(hb.T, u, vt, sa, sb, ea, eb, tau)
    return x + out_t.T


def _rms(x, g):
    return x * lax.rsqrt(jnp.mean(x * x, axis=-1, keepdims=True) + EPS) * g


def _rope(x, pos, n_rot, theta):
    half = n_rot // 2
    inv = theta ** (-(jnp.arange(half, dtype=F32) * 2.0 / n_rot))
    ang = pos.astype(F32)[:, None] * inv[None, :]
    shape = (pos.shape[0],) + (1,) * (x.ndim - 2) + (half,)
    cos = jnp.cos(ang).reshape(shape)
    sin = jnp.sin(ang).reshape(shape)
    x1, x2, rest = x[..., :half], x[..., half:n_rot], x[..., n_rot:]
    return jnp.concatenate([x1 * cos - x2 * sin, x1 * sin + x2 * cos, rest], axis=-1)


def _pad_rows(x, rows):
    return jnp.pad(x, ((0, 0), (0, rows - x.shape[1]), (0, 0)))


def _shifted_rows(x, rows):
    t = x.shape[1]
    return jnp.stack([jnp.pad(x, ((0, 0), (d, rows - t - d), (0, 0))) for d in range(t)], axis=1)


def _even_layer(x, pos, n_p, bp, tp_len, bs, ts_len, g_mix, w_in, w_out, qn, kn, a1, a2, a3, ret_state, e):
    hw = A_HEADS * DH
    n = x.shape[0]
    proj = norm_matmul(x, g_mix, w_in.astype(BF16), 1536)
    qa, ka, va = proj[:, :3 * hw], proj[:, 3 * hw:6 * hw], proj[:, 6 * hw:9 * hw]
    o = 9 * hw
    qr, kr = proj[:, o:o + 256], proj[:, o + 256:o + 512]
    vr, gr = proj[:, o + 512:o + 1024], proj[:, o + 1024:o + 1536]
    hs = (n, 3, A_HEADS, DH)
    qa = _rope(_rms(qa.reshape(hs), qn[:, None, :]), pos, ROPE_DIMS, ROPE_THETA).reshape(n, 3 * hw)
    ka = _rope(_rms(ka.reshape(hs), kn[:, None, :]), pos, ROPE_DIMS, ROPE_THETA).reshape(n, 3 * hw)
    qs = (qa * (DH ** -0.5)).astype(BF16)
    shp = (bp, tp_len, 3 * hw)
    oa_p = dilated_prompt(qs[:n_p].reshape(shp), ka[:n_p].astype(BF16).reshape(shp),
                          va[:n_p].astype(BF16).reshape(shp))
    ka_p = ka[:n_p].reshape(bp, tp_len, 3, A_HEADS, DH)
    va_p = va[:n_p].reshape(bp, tp_len, 3, A_HEADS, DH)
    new_p = []
    for gi, win in enumerate(A_WINDOWS):
        keep = min(win, tp_len)
        new_p.append(jnp.stack([ka_p[:, tp_len - keep:, gi], va_p[:, tp_len - keep:, gi]], axis=2))
    shs = (bs, ts_len, 3 * hw)
    ka_s, va_s = ka[n_p:].reshape(shs), va[n_p:].reshape(shs)
    oa_s = dilated_sample(_pad_rows(qs[n_p:].reshape(shs), SAMPLE_PAD),
                          _shifted_rows(ka_s, SAMPLE_PAD), _shifted_rows(va_s, SAMPLE_PAD),
                          a1, a2, a3, e)[:, :ts_len]
    ka_s5 = ka_s.reshape(bs, ts_len, 3, A_HEADS, DH)
    va_s5 = va_s.reshape(bs, ts_len, 3, A_HEADS, DH)
    new_s = [jnp.stack([ka_s5[:, :, gi], va_s5[:, :, gi]], axis=2) for gi in range(3)]
    oa = jnp.concatenate([oa_p.reshape(n_p, hw), oa_s.reshape(n - n_p, hw)], axis=0)
    qr = _rope(qr.reshape(n, B_HEADS, B_DK), pos, B_DK, RET_THETA) * (B_DK ** -0.5)
    kr = _rope(kr.reshape(n, B_HEADS, B_DK), pos, B_DK, RET_THETA)
    qr, kr = qr.reshape(n, B_HEADS * B_DK), kr.reshape(n, B_HEADS * B_DK)
    log_g = jnp.log1p(-jnp.exp2(-5.0 - jnp.arange(B_HEADS, dtype=F32)))
    s0 = jnp.zeros((bp, B_HEADS, B_DK, B_DV), F32)
    o_p, ret_p = decay_scan(qr[:n_p].reshape(bp, tp_len, -1), kr[:n_p].reshape(bp, tp_len, -1),
                            vr[:n_p].reshape(bp, tp_len, -1),
                            jnp.broadcast_to(log_g, (bp, tp_len, B_HEADS)), s0, CHUNK)
    la_s = jnp.broadcast_to(log_g, (bs, ts_len, B_HEADS))
    o_s, ret_s = decay_scan(_pad_rows(qr[n_p:].reshape(bs, ts_len, -1), SAMPLE_PAD),
                            _pad_rows(kr[n_p:].reshape(bs, ts_len, -1), SAMPLE_PAD),
                            _pad_rows(vr[n_p:].reshape(bs, ts_len, -1), SAMPLE_PAD),
                            _pad_rows(la_s, SAMPLE_PAD), ret_state, SAMPLE_PAD)
    o_r = jnp.concatenate([o_p.reshape(n_p, B_HEADS, B_DV),
                           o_s[:, :ts_len].reshape(n - n_p, B_HEADS, B_DV)], axis=0)
    mu = jnp.mean(o_r, axis=-1, keepdims=True)
    var = jnp.mean(jnp.square(o_r - mu), axis=-1, keepdims=True)
    o_r = (o_r - mu) * lax.rsqrt(var + EPS)
    ob = o_r.reshape(n, B_HEADS * B_DV) * jax.nn.silu(gr)
    cat = jnp.concatenate([oa, ob], axis=-1).astype(BF16)
    x = matmul_res(cat, w_out.astype(BF16), x)
    return x, new_p, new_s, ret_p, ret_s


def _odd_layer(x, pos, n_p, bp, tp_len, bs, ts_len, lam_init, g_mix, w_in, w_out, conv_w, conv_b, dt_bias,
               a_log, d_skip, ssm_norm, d_qn, d_kn, lq1, lk1, lq2, lk2, d_subln,
               conv_buf, ssm_state, cache_d, page_table, o_idx):
    n = x.shape[0]
    dc = D_HEADS * 2 * D_DH
    offs = [0, D_INNER, D_INNER + CONV_CH, D_INNER + CONV_CH + C_HEADS]
    w_z = w_in[:, :offs[1]]
    w_xbc = w_in[:, offs[1]:offs[2]]
    w_dt = w_in[:, offs[2]:offs[3]]
    w_qkv = w_in[:, offs[3]:]
    w_perm = jnp.concatenate([w_z, w_xbc, w_qkv, jnp.pad(w_dt, ((0, 0), (0, 128 - C_HEADS)))], axis=1)
    proj = norm_matmul(x, g_mix, w_perm.astype(BF16), w_perm.shape[1] // 3)
    z = proj[:, :D_INNER]
    xbc = proj[:, D_INNER:D_INNER + CONV_CH]
    o = D_INNER + CONV_CH
    qd, kd, vd = proj[:, o:o + dc], proj[:, o + dc:o + 2 * dc], proj[:, o + 2 * dc:o + 3 * dc]
    dtr = proj[:, o + 3 * dc:o + 3 * dc + C_HEADS]
    xbc_p = xbc[:n_p].reshape(bp, tp_len, CONV_CH)
    xbc_s = xbc[n_p:].reshape(bs, ts_len, CONV_CH)

    def conv(u, buf):
        t = u.shape[1]
        up = jnp.concatenate([buf, u], axis=1)
        out = conv_b
        for i in range(CONV_W):
            out = out + up[:, i:i + t] * conv_w[i]
        return out, up[:, t:]

    cp, new_conv_p = conv(xbc_p, jnp.zeros((bp, CONV_W - 1, CONV_CH), F32))
    cs, new_conv_s = conv(xbc_s, conv_buf)
    xbc = jax.nn.silu(jnp.concatenate([cp.reshape(n_p, CONV_CH), cs.reshape(n - n_p, CONV_CH)], axis=0))
    xs_ = xbc[:, :D_INNER]
    bm = xbc[:, D_INNER:D_INNER + C_GROUPS * N_STATE]
    cm = xbc[:, D_INNER + C_GROUPS * N_STATE:]
    dt = jax.nn.softplus(dtr + dt_bias)
    log_a = dt * (-jnp.exp(a_log))
    xs_h = xs_.reshape(n, C_HEADS, C_HEADDIM)
    vin = (xs_h * dt[..., None]).reshape(n, D_INNER)
    s0 = jnp.zeros((bp, C_HEADS, N_STATE, C_HEADDIM), F32)
    yo_p, ssm_p = decay_scan(cm[:n_p].reshape(bp, tp_len, -1), bm[:n_p].reshape(bp, tp_len, -1),
                             vin[:n_p].reshape(bp, tp_len, -1), log_a[:n_p].reshape(bp, tp_len, -1), s0, CHUNK)
    yo_s, ssm_s = decay_scan(_pad_rows(cm[n_p:].reshape(bs, ts_len, -1), SAMPLE_PAD),
                             _pad_rows(bm[n_p:].reshape(bs, ts_len, -1), SAMPLE_PAD),
                             _pad_rows(vin[n_p:].reshape(bs, ts_len, -1), SAMPLE_PAD),
                             _pad_rows(log_a[n_p:].reshape(bs, ts_len, -1), SAMPLE_PAD), ssm_state, SAMPLE_PAD)
    yo = jnp.concatenate([yo_p.reshape(n_p, D_INNER), yo_s[:, :ts_len].reshape(n - n_p, D_INNER)], axis=0)
    y = yo.reshape(n, C_HEADS, C_HEADDIM) + d_skip[:, None] * xs_h
    y = _rms(y.reshape(n, D_INNER) * jax.nn.silu(z), ssm_norm)
    qd = _rope(_rms(qd.reshape(n, D_HEADS, 2, D_DH), d_qn), pos, ROPE_DIMS, ROPE_THETA).reshape(n, dc)
    kd = _rope(_rms(kd.reshape(n, D_HEADS, 2, D_DH), d_kn), pos, ROPE_DIMS, ROPE_THETA).reshape(n, dc)
    lam = jnp.exp(jnp.sum(lq1 * lk1)) - jnp.exp(jnp.sum(lq2 * lk2)) + lam_init
    qs = (qd * (D_DH ** -0.5)).astype(BF16)
    shp = (bp, tp_len, dc)
    od_p = diff_prompt(lam, qs[:n_p].reshape(shp), kd[:n_p].astype(BF16).reshape(shp),
                       vd[:n_p].astype(BF16).reshape(shp))
    shs = (bs, ts_len, dc)
    kd_s, vd_s = kd[n_p:].reshape(shs), vd[n_p:].reshape(shs)
    od_s = diff_sample(page_table, lam, _pad_rows(qs[n_p:].reshape(shs), SAMPLE_PAD),
                       _shifted_rows(kd_s, SAMPLE_PAD), _shifted_rows(vd_s, SAMPLE_PAD),
                       cache_d, o_idx)[:, :ts_len]
    od = jnp.concatenate([od_p.reshape(n_p, dc), od_s.reshape(n - n_p, dc)], axis=0)
    od = (_rms(od.reshape(n, D_HEADS, D_DV), d_subln) * (1.0 - lam_init)).reshape(n, dc)
    new_d_p = jnp.stack([kd[:n_p].reshape(bp, tp_len, D_HEADS, D_DV),
                         vd[:n_p].reshape(bp, tp_len, D_HEADS, D_DV)], axis=2)
    new_d_s = jnp.stack([kd_s.reshape(bs, ts_len, D_HEADS, D_DV),
                         vd_s.reshape(bs, ts_len, D_HEADS, D_DV)], axis=2)
    cat = jnp.concatenate([y, od], axis=-1).astype(BF16)
    x = matmul_res(cat, w_out.astype(BF16), x)
    return x, new_conv_p, new_conv_s, ssm_p, ssm_s, new_d_p, new_d_s


def kernel(x_prompt, x_sample, cache_a1, cache_a2, cache_a3, state_ret, state_ssm, state_conv, cache_d, page_table, norm_mix, norm_ffn, w_in_even, w_out_even, a_qnorm, a_knorm, w_in_odd, w_out_odd, conv_w, conv_b, dt_bias, a_log, d_skip, ssm_norm, d_qnorm, d_knorm, lam_q1, lam_k1, lam_q2, lam_k2, d_subln, peer_wq, peer_ka, peer_kb, peer_u, peer_v):
    bp, tp_len, d = x_prompt.shape
    bs, ts_len, _ = x_sample.shape
    past = cache_a3.shape[2]
    depth = norm_mix.shape[0]
    n_p = bp * tp_len
    x = jnp.concatenate([x_prompt.reshape(n_p, d), x_sample.reshape(bs * ts_len, d)], axis=0)
    pos = jnp.concatenate([jnp.tile(jnp.arange(tp_len, dtype=jnp.int32), bp),
                           jnp.tile(past + jnp.arange(ts_len, dtype=jnp.int32), bs)])
    a_p, a_s = ([], [], []), ([], [], [])
    ret_p, ret_s, ssm_p, ssm_s, conv_p, conv_s, d_p, d_s = [], [], [], [], [], [], [], []
    for l in range(depth):
        if l % 2 == 0:
            e = l // 2
            x, nap, nas, rp, rs = _even_layer(
                x, pos, n_p, bp, tp_len, bs, ts_len, norm_mix[l], w_in_even[e], w_out_even[e],
                a_qnorm[e], a_knorm[e], cache_a1, cache_a2, cache_a3, state_ret[e], e)
            for gi in range(3):
                a_p[gi].append(nap[gi])
                a_s[gi].append(nas[gi])
            ret_p.append(rp)
            ret_s.append(rs)
        else:
            o = l // 2
            lam_init = 0.8 - 0.6 * math.exp(-0.3 * l)
            x, cp, cs, sp, ss, dp, ds = _odd_layer(
                x, pos, n_p, bp, tp_len, bs, ts_len, lam_init, norm_mix[l], w_in_odd[o], w_out_odd[o],
                conv_w[o], conv_b[o], dt_bias[o], a_log[o], d_skip[o], ssm_norm[o], d_qnorm[o], d_knorm[o],
                lam_q1[o], lam_k1[o], lam_q2[o], lam_k2[o], d_subln[o],
                state_conv[o], state_ssm[o], cache_d, page_table, o)
            conv_p.append(cp)
            conv_s.append(cs)
            ssm_p.append(sp)
            ssm_s.append(ss)
            d_p.append(dp)
            d_s.append(ds)
        x = peer(x, norm_ffn[l], peer_wq[l].astype(BF16), peer_ka[l].astype(BF16), peer_kb[l].astype(BF16),
                 peer_u[l].astype(BF16), peer_v[l].T.astype(BF16))
    y_p = x[:n_p].reshape(bp, tp_len, d)
    y_s = x[n_p:].reshape(bs, ts_len, d)
    return (y_p, y_s,
            jnp.stack(a_p[0], 0), jnp.stack(a_s[0], 0),
            jnp.stack(a_p[1], 0), jnp.stack(a_s[1], 0),
            jnp.stack(a_p[2], 0), jnp.stack(a_s[2], 0),
            jnp.stack(ret_p, 0), jnp.stack(ret_s, 0),
            jnp.stack(ssm_p, 0), jnp.stack(ssm_s, 0),
            jnp.stack(conv_p, 0), jnp.stack(conv_s, 0),
            jnp.stack(d_p, 0), jnp.stack(d_s, 0))
```
